```python
import math
import jax, jax.numpy as jnp
from jax import lax
import numpy as np

D_MODEL = 1024
BATCH = 8
SEQ = 4096
DEPTH = 4

N_EVEN = (DEPTH + 1) // 2
N_ODD = DEPTH // 2

D_A = D_MODEL // 2
A_CONV = 3
D_B = D_MODEL // 2
POOL_WINDOWS = (2, 4, 8, 16)
POOL_GROUPS = len(POOL_WINDOWS)
POOL_GC = D_B // POOL_GROUPS
EV_IN = 3 * D_A + D_B
EV_MIX = D_A + D_B

D_C = D_MODEL // 2
HEAD_DIM = 64
H_C = D_C // HEAD_DIM
Q_BLOCK = 128
D_D = D_MODEL // 2
D_CONV = 31
OD_IN = 3 * D_C + H_C + 2 * D_D
OD_MIX = D_C + D_D

D_FF = int(math.ceil((8 * D_MODEL / 3) / 256) * 256)

ALPHA = (2.0 * DEPTH) ** 0.25
BETA = (8.0 * DEPTH) ** -0.25
LN_EPS = 1e-5

kernel_name = "hybrid_shortconv_pool_fox_conformer_deepnorm"


def layer_norm(x, g, b):
    x32 = x.astype(jnp.float32)
    mu = jnp.mean(x32, axis=-1, keepdims=True)
    var = jnp.mean(jnp.square(x32 - mu), axis=-1, keepdims=True)
    y = (x32 - mu) * lax.rsqrt(var + LN_EPS)
    return (y * g.astype(jnp.float32) + b.astype(jnp.float32)).astype(x.dtype)


def causal_dwconv(x, w):
    k = w.shape[0]
    return lax.conv_general_dilated(
        x, w[:, None, :].astype(x.dtype), window_strides=(1,), padding=[(k - 1, 0)],
        dimension_numbers=("NWC", "WIO", "NWC"), feature_group_count=x.shape[-1])


def short_gated_conv(p_a, conv_w):
    b_gate = p_a[..., :D_A]
    c_gate = p_a[..., D_A:2 * D_A]
    val = p_a[..., 2 * D_A:]
    return b_gate * causal_dwconv(c_gate * val, conv_w)


def multiscale_pool(u, w_grp, scale):
    bsz, s, _ = u.shape
    u32 = u.astype(jnp.float32).reshape(bsz, s, POOL_GROUPS, POOL_GC)
    cs = jnp.cumsum(u32, axis=1)
    pos = jnp.arange(s)
    pooled = []
    for g, win in enumerate(POOL_WINDOWS):
        c = cs[:, :, g]
        lagged = jnp.pad(c, ((0, 0), (win, 0), (0, 0)))[:, :s]
        cnt = jnp.minimum(pos + 1, win).astype(jnp.float32)[None, :, None]
        pooled.append((c - lagged) / cnt)
    pooled = (jnp.stack(pooled, axis=2) - u32).astype(u.dtype)
    y = jnp.einsum("bsgc,gcd->bsgd", pooled, w_grp).reshape(bsz, s, D_B)
    return y * scale


def forgetting_attention(q, k, v, log_f):
    s_len = q.shape[1]
    cum = jnp.cumsum(log_f.astype(jnp.float32), axis=1).transpose(0, 2, 1)
    scale = 1.0 / math.sqrt(HEAD_DIM)
    outs = []
    for blk in range(s_len // Q_BLOCK):
        q0, q1 = blk * Q_BLOCK, (blk + 1) * Q_BLOCK
        logits = jnp.einsum("bqhd,bkhd->bhqk", q[:, q0:q1], k[:, :q1]).astype(jnp.float32) * scale
        logits = logits + cum[:, :, q0:q1, None] - cum[:, :, None, :q1]
        mask = jnp.arange(q0, q1)[:, None] >= jnp.arange(q1)[None, :]
        logits = jnp.where(mask[None, None], logits, -1e30)
        probs = jax.nn.softmax(logits, axis=-1).astype(v.dtype)
        outs.append(jnp.einsum("bhqk,bkhd->bqhd", probs, v[:, :q1]))
    return jnp.concatenate(outs, axis=1)


def conformer_conv(p_d, dw_w, dw_b, cn_g, cn_b):
    h = p_d[..., :D_D] * jax.nn.sigmoid(p_d[..., D_D:])
    h = causal_dwconv(h, dw_w) + dw_b
    h = layer_norm(h, cn_g, cn_b)
    return jax.nn.silu(h)


def even_mixer(x, w_in, conv_w, pool_w, pool_scale, w_out):
    p = x @ w_in
    y_a = short_gated_conv(p[..., :3 * D_A], conv_w)
    y_b = multiscale_pool(p[..., 3 * D_A:], pool_w, pool_scale)
    return jnp.concatenate([y_a, y_b], axis=-1) @ w_out


def odd_mixer(x, w_in, forget_b, dw_w, dw_b, cn_g, cn_b, w_out):
    bsz, s, _ = x.shape
    p = x @ w_in
    q = p[..., :D_C].reshape(bsz, s, H_C, HEAD_DIM)
    k = p[..., D_C:2 * D_C].reshape(bsz, s, H_C, HEAD_DIM)
    v = p[..., 2 * D_C:3 * D_C].reshape(bsz, s, H_C, HEAD_DIM)
    log_f = jax.nn.log_sigmoid((p[..., 3 * D_C:3 * D_C + H_C] + forget_b).astype(jnp.float32))
    y_c = forgetting_attention(q, k, v, log_f).reshape(bsz, s, D_C)
    y_d = conformer_conv(p[..., 3 * D_C + H_C:], dw_w, dw_b, cn_g, cn_b)
    return jnp.concatenate([y_c, y_d], axis=-1) @ w_out


def swiglu(x, w_in, w_out):
    h = x @ w_in
    return (jax.nn.silu(h[..., :D_FF]) * h[..., D_FF:]) @ w_out


def setup_inputs(seed: int = 0) -> dict:
    key = jax.random.key(seed)
    ks = jax.random.split(key, 24)
    f32 = jnp.float32
    nrm = lambda k, shape, s: jax.random.normal(k, shape, f32) * s
    x = jax.random.normal(ks[0], (BATCH, SEQ, D_MODEL), f32)
    ev_w_in = nrm(ks[1], (N_EVEN, D_MODEL, EV_IN), D_MODEL ** -0.5)
    ev_conv_w = nrm(ks[2], (N_EVEN, A_CONV, D_A), A_CONV ** -0.5)
    ev_pool_w = nrm(ks[3], (N_EVEN, POOL_GROUPS, POOL_GC, POOL_GC), POOL_GC ** -0.5)
    ev_pool_scale = 1.0 + nrm(ks[4], (N_EVEN, D_B), 0.1)
    ev_w_out = nrm(ks[5], (N_EVEN, EV_MIX, D_MODEL), EV_MIX ** -0.5 * BETA)
    col_scale = jnp.ones((OD_IN,), f32).at[2 * D_C:3 * D_C].set(BETA)
    od_w_in = nrm(ks[6], (N_ODD, D_MODEL, OD_IN), D_MODEL ** -0.5) * col_scale
    od_forget_b = jax.random.uniform(ks[7], (N_ODD, H_C), f32, 1.0, 4.0)
    od_dw_w = nrm(ks[8], (N_ODD, D_CONV, D_D), D_CONV ** -0.5)
    od_dw_b = nrm(ks[9], (N_ODD, D_D), 0.02)
    od_cn_g = 1.0 + nrm(ks[10], (N_ODD, D_D), 0.05)
    od_cn_b = nrm(ks[11], (N_ODD, D_D), 0.02)
    od_w_out = nrm(ks[12], (N_ODD, OD_MIX, D_MODEL), OD_MIX ** -0.5 * BETA)
    ln_mix_g = 1.0 + nrm(ks[13], (DEPTH, D_MODEL), 0.05)
    ln_mix_b = nrm(ks[14], (DEPTH, D_MODEL), 0.02)
    ln_ffn_g = 1.0 + nrm(ks[15], (DEPTH, D_MODEL), 0.05)
    ln_ffn_b = nrm(ks[16], (DEPTH, D_MODEL), 0.02)
    ffn_w_in = nrm(ks[17], (DEPTH, D_MODEL, 2 * D_FF), D_MODEL ** -0.5)
    ffn_w_out = nrm(ks[18], (DEPTH, D_FF, D_MODEL), D_FF ** -0.5 * BETA)
    return {"x": x, "ev_w_in": ev_w_in, "ev_conv_w": ev_conv_w, "ev_pool_w": ev_pool_w,
            "ev_pool_scale": ev_pool_scale, "ev_w_out": ev_w_out, "od_w_in": od_w_in,
            "od_forget_b": od_forget_b, "od_dw_w": od_dw_w, "od_dw_b": od_dw_b,
            "od_cn_g": od_cn_g, "od_cn_b": od_cn_b, "od_w_out": od_w_out,
            "ln_mix_g": ln_mix_g, "ln_mix_b": ln_mix_b, "ln_ffn_g": ln_ffn_g,
            "ln_ffn_b": ln_ffn_b, "ffn_w_in": ffn_w_in, "ffn_w_out": ffn_w_out}


def reference(x, ev_w_in, ev_conv_w, ev_pool_w, ev_pool_scale, ev_w_out, od_w_in,
              od_forget_b, od_dw_w, od_dw_b, od_cn_g, od_cn_b, od_w_out,
              ln_mix_g, ln_mix_b, ln_ffn_g, ln_ffn_b, ffn_w_in, ffn_w_out):
    for layer in range(DEPTH):
        i = layer // 2
        if layer % 2 == 0:
            mix = even_mixer(x, ev_w_in[i], ev_conv_w[i], ev_pool_w[i], ev_pool_scale[i], ev_w_out[i])
        else:
            mix = odd_mixer(x, od_w_in[i], od_forget_b[i], od_dw_w[i], od_dw_b[i],
                            od_cn_g[i], od_cn_b[i], od_w_out[i])
        x = layer_norm(ALPHA * x + mix, ln_mix_g[layer], ln_mix_b[layer])
        x = layer_norm(ALPHA * x + swiglu(x, ffn_w_in[layer], ffn_w_out[layer]),
                       ln_ffn_g[layer], ln_ffn_b[layer])
    return x
```

```python
import functools
import math

import jax
import jax.numpy as jnp
from jax import lax
from jax.experimental import pallas as pl
from jax.experimental.pallas import tpu as pltpu

F32 = jnp.float32
BF16 = jnp.bfloat16

D_MODEL = 1024
DEPTH = 4
D_A = D_MODEL // 2
A_CONV = 3
D_B = D_MODEL // 2
POOL_WINDOWS = (2, 4, 8, 16)
POOL_GC = D_B // len(POOL_WINDOWS)
D_C = D_MODEL // 2
HEAD_DIM = 64
H_C = D_C // HEAD_DIM
D_D = D_MODEL // 2
D_CONV = 31
D_FF = int(math.ceil((8 * D_MODEL / 3) / 256) * 256)
ALPHA = (2.0 * DEPTH) ** 0.25
LN_EPS = 1e-5
MASK_VALUE = -1e30

LANES = 128
ROW_TILE = 512
ATTN_TILE = 512
FFN_CHUNK = 256
EVEN_HALO = 16
CONV_HALO = 32
F_PAD = LANES
OD_COLS = 3 * D_C + 2 * D_D + F_PAD
VMEM_LIMIT = 56 * 1024 * 1024


def _layer_norm(y, g, b):
    mu = jnp.mean(y, axis=-1, keepdims=True)
    d = y - mu
    var = jnp.mean(d * d, axis=-1, keepdims=True)
    return d * lax.rsqrt(var + LN_EPS) * g + b


def _dot(a, b):
    return jnp.dot(a, b, preferred_element_type=F32)


def _const_spec(shape, layer=None):
    if layer is None:
        return pl.BlockSpec(shape, lambda *_: (0,) * len(shape))
    return pl.BlockSpec((None,) + shape, lambda *_: (layer,) + (0,) * len(shape))


def _params(n_axes):
    return pltpu.CompilerParams(
        dimension_semantics=("arbitrary",) * n_axes, vmem_limit_bytes=VMEM_LIMIT)


def _ffn_kernel(x_ref, win_ref, wout_ref, g_ref, b_ref, o_ref, act_ref):
    x = x_ref[...]
    xb = x.astype(BF16)
    for c in range(D_FF // FFN_CHUNK):
        lo, hi = c * FFN_CHUNK, (c + 1) * FFN_CHUNK
        gate = _dot(xb, win_ref[:, lo:hi])
        up = _dot(xb, win_ref[:, D_FF + lo:D_FF + hi])
        act_ref[:, lo:hi] = (gate * jax.nn.sigmoid(gate) * up).astype(BF16)
    y = ALPHA * x + _dot(act_ref[...], wout_ref[...])
    o_ref[...] = _layer_norm(y, g_ref[...], b_ref[...])


def _ffn(x, w_in, w_out, g, b, layer):
    n = x.shape[0]
    row = pl.BlockSpec((ROW_TILE, D_MODEL), lambda i: (i, 0))
    return pl.pallas_call(
        _ffn_kernel,
        grid=(n // ROW_TILE,),
        in_specs=[row, _const_spec((D_MODEL, 2 * D_FF), layer), _const_spec((D_FF, D_MODEL), layer),
                  _const_spec((1, D_MODEL), layer), _const_spec((1, D_MODEL), layer)],
        out_specs=row,
        out_shape=jax.ShapeDtypeStruct((n, D_MODEL), F32),
        scratch_shapes=[pltpu.VMEM((ROW_TILE, D_FF), BF16)],
        compiler_params=_params(1),
        name="ffn",
    )(x, w_in, w_out, g, b)


def _carry_halo(ext_ref, first, halo, tm):
    @pl.when(first)
    def _():
        ext_ref[0:halo, :] = jnp.zeros((halo, ext_ref.shape[1]), ext_ref.dtype)

    @pl.when(jnp.logical_not(first))
    def _():
        ext_ref[0:halo, :] = ext_ref[tm:tm + halo, :]


def _even_kernel(x_ref, win_ref, convw_ref, poolw_ref, pscale_ref, wout_ref, g_ref, b_ref,
                 o_ref, ecv_ref, eu_ref, *, tiles_per_seq):
    tm, h0 = ROW_TILE, EVEN_HALO
    tile_in_seq = pl.program_id(0) % tiles_per_seq
    first = tile_in_seq == 0
    _carry_halo(ecv_ref, first, h0, tm)
    _carry_halo(eu_ref, first, h0, tm)

    x = x_ref[...]
    p = _dot(x.astype(BF16), win_ref[...])
    b_gate = p[:, :D_A]
    ecv_ref[h0:h0 + tm, :] = p[:, D_A:2 * D_A] * p[:, 2 * D_A:3 * D_A]
    eu_ref[h0:h0 + tm, :] = p[:, 3 * D_A:]

    conv = convw_ref[A_CONV - 1:A_CONV, :] * ecv_ref[h0:h0 + tm, :]
    for k in range(1, A_CONV):
        conv = conv + convw_ref[A_CONV - 1 - k:A_CONV - k, :] * ecv_ref[h0 - k:h0 - k + tm, :]
    y_a = (b_gate * conv).astype(BF16)

    pos1 = lax.broadcasted_iota(jnp.int32, (tm, POOL_GC), 0) + (tile_in_seq * tm + 1)
    y_b = []
    for gi, win in enumerate(POOL_WINDOWS):
        sl = slice(gi * POOL_GC, (gi + 1) * POOL_GC)
        cur = eu_ref[h0:h0 + tm, sl]
        tot = cur
        for k in range(1, win):
            tot = tot + eu_ref[h0 - k:h0 - k + tm, sl]
        cnt = jnp.minimum(pos1, win).astype(F32)
        pooled = (tot / cnt - cur).astype(BF16)
        y_b.append(_dot(pooled, poolw_ref[gi]) * pscale_ref[:, sl])
    y_b = jnp.concatenate(y_b, axis=-1).astype(BF16)

    mix = _dot(y_a, wout_ref[:D_A, :]) + _dot(y_b, wout_ref[D_A:, :])
    o_ref[...] = _layer_norm(ALPHA * x + mix, g_ref[...], b_ref[...])


def _even_layer(x, w_in, conv_w, pool_w, pool_scale, w_out, g, b, i, layer, seq):
    n = x.shape[0]
    row = pl.BlockSpec((ROW_TILE, D_MODEL), lambda t: (t, 0))
    return pl.pallas_call(
        functools.partial(_even_kernel, tiles_per_seq=seq // ROW_TILE),
        grid=(n // ROW_TILE,),
        in_specs=[row, _const_spec((D_MODEL, 3 * D_A + D_B), i), _const_spec((A_CONV, D_A), i),
                  _const_spec((len(POOL_WINDOWS), POOL_GC, POOL_GC), i), _const_spec((1, D_B), i),
                  _const_spec((D_A + D_B, D_MODEL), i),
                  _const_spec((1, D_MODEL), layer), _const_spec((1, D_MODEL), layer)],
        out_specs=row,
        out_shape=jax.ShapeDtypeStruct((n, D_MODEL), F32),
        scratch_shapes=[pltpu.VMEM((EVEN_HALO + ROW_TILE, D_A), F32),
                        pltpu.VMEM((EVEN_HALO + ROW_TILE, D_B), F32)],
        compiler_params=_params(1),
        name="even_mixer",
    )(x, w_in, conv_w, pool_w, pool_scale, w_out, g, b)


def _odd_in_kernel(x_ref, w_ref, fb_ref, dww_ref, dwb_ref, cng_ref, cnb_ref,
                   q_ref, k_ref, v_ref, cum_ref, yd_ref, eh_ref, carry_ref, *, tiles_per_seq):
    tm, h0 = ROW_TILE, CONV_HALO
    first = pl.program_id(0) % tiles_per_seq == 0
    _carry_halo(eh_ref, first, h0, tm)

    @pl.when(first)
    def _():
        carry_ref[...] = jnp.zeros(carry_ref.shape, F32)

    p = _dot(x_ref[...].astype(BF16), w_ref[...])
    q_ref[...] = p[:, :D_C].astype(BF16)
    k_ref[...] = p[:, D_C:2 * D_C].astype(BF16)
    v_ref[...] = p[:, 2 * D_C:3 * D_C].astype(BF16)

    c0 = 3 * D_C + 2 * D_D
    cum = jax.nn.log_sigmoid(p[:, c0:c0 + F_PAD] + fb_ref[...])
    rows = lax.broadcasted_iota(jnp.int32, (tm, F_PAD), 0)
    d = 1
    while d < tm:
        cum = cum + jnp.where(rows >= d, pltpu.roll(cum, d, axis=0), 0.0)
        d *= 2
    cum = cum + carry_ref[...]
    cum_ref[...] = cum
    carry_ref[...] = cum[tm - 1:tm, :]

    a0 = 3 * D_C
    eh_ref[h0:h0 + tm, :] = p[:, a0:a0 + D_D] * jax.nn.sigmoid(p[:, a0 + D_D:a0 + 2 * D_D])
    acc = dwb_ref[...] + dww_ref[D_CONV - 1:D_CONV, :] * eh_ref[h0:h0 + tm, :]
    for k in range(1, D_CONV):
        acc = acc + dww_ref[D_CONV - 1 - k:D_CONV - k, :] * eh_ref[h0 - k:h0 - k + tm, :]
    hn = _layer_norm(acc, cng_ref[...], cnb_ref[...])
    yd_ref[...] = (hn * jax.nn.sigmoid(hn)).astype(BF16)


def _odd_in(x, w, fb, dw_w, dw_b, cn_g, cn_b, i, seq):
    n = x.shape[0]
    row = lambda width: pl.BlockSpec((ROW_TILE, width), lambda t: (t, 0))
    half = jax.ShapeDtypeStruct((n, D_C), BF16)
    return pl.pallas_call(
        functools.partial(_odd_in_kernel, tiles_per_seq=seq // ROW_TILE),
        grid=(n // ROW_TILE,),
        in_specs=[row(D_MODEL), _const_spec((D_MODEL, OD_COLS), i), _const_spec((1, F_PAD), i),
                  _const_spec((D_CONV, D_D), i), _const_spec((1, D_D), i),
                  _const_spec((1, D_D), i), _const_spec((1, D_D), i)],
        out_specs=[row(D_C), row(D_C), row(D_C), row(F_PAD), row(D_D)],
        out_shape=[half, half, half, jax.ShapeDtypeStruct((n, F_PAD), F32), half],
        scratch_shapes=[pltpu.VMEM((CONV_HALO + ROW_TILE, D_D), F32), pltpu.VMEM((1, F_PAD), F32)],
        compiler_params=_params(1),
        name="odd_in",
    )(x, w, fb, dw_w, dw_b, cn_g, cn_b)


def _attn_kernel(q_ref, k_ref, v_ref, cumc_ref, cumr_ref, o_ref, m_ref, acc_ref, *, n_kv):
    tq = ATTN_TILE
    qi = pl.program_id(1)
    low = lax.broadcasted_iota(jnp.int32, (tq, LANES), 1) < HEAD_DIM
    causal = (lax.broadcasted_iota(jnp.int32, (tq, tq), 0)
              >= lax.broadcasted_iota(jnp.int32, (tq, tq), 1))
    own = (low, jnp.logical_not(low))

    for pair in range(H_C // 2):
        sl = slice(pair * LANES, (pair + 1) * LANES)
        qp = q_ref[:, sl]
        qs = [jnp.where(own[j], qp, jnp.zeros_like(qp)) for j in range(2)]
        cq = [cumc_ref[:, 2 * pair + j:2 * pair + j + 1] for j in range(2)]
        for j in range(2):
            m_ref[j] = jnp.full((tq, 1), MASK_VALUE, F32)
            acc_ref[j] = jnp.zeros((tq, LANES), F32)

        def step(kv, masked, sl=sl, qs=qs, cq=cq, pair=pair):
            start = pl.multiple_of(kv * tq, tq)
            kb = k_ref[pl.ds(start, tq), sl]
            vb = v_ref[pl.ds(start, tq), sl]
            for j in range(2):
                s = lax.dot_general(qs[j], kb, (((1,), (1,)), ((), ())),
                                    preferred_element_type=F32)
                s = s + cq[j] - cumr_ref[(2 * pair + j) * n_kv + kv]
                if masked:
                    s = jnp.where(causal, s, MASK_VALUE)
                m_old = m_ref[j]
                m_new = jnp.maximum(m_old, jnp.max(s, axis=-1, keepdims=True))
                prob = jnp.exp(s - m_new).astype(BF16)
                vm = jnp.where(own[j], vb, jnp.ones_like(vb))
                acc_ref[j] = acc_ref[j] * jnp.exp(m_old - m_new) + _dot(prob, vm)
                m_ref[j] = m_new

        def body(kv, carry):
            step(kv, False)
            return carry

        lax.fori_loop(0, qi, body, 0)
        step(qi, True)

        outs = []
        for j in range(2):
            a = acc_ref[j]
            outs.append(a / pltpu.roll(a, HEAD_DIM, axis=1))
        o_ref[:, sl] = jnp.where(low, outs[0], outs[1]).astype(BF16)


def _attention(q, k, v, cum_cols, cum_rows, batch, seq):
    n = q.shape[0]
    n_kv = seq // ATTN_TILE
    qrow = lambda width: pl.BlockSpec((ATTN_TILE, width), lambda b, t: (b * n_kv + t, 0))
    kv_spec = pl.BlockSpec((seq, D_C), lambda b, t: (b, 0))
    return pl.pallas_call(
        functools.partial(_attn_kernel, n_kv=n_kv),
        grid=(batch, n_kv),
        in_specs=[qrow(D_C), kv_spec, kv_spec, qrow(F_PAD),
                  pl.BlockSpec((None, H_C * n_kv, 1, ATTN_TILE), lambda b, t: (b, 0, 0, 0))],
        out_specs=qrow(D_C),
        out_shape=jax.ShapeDtypeStruct((n, D_C), BF16),
        scratch_shapes=[pltpu.VMEM((2, ATTN_TILE, 1), F32), pltpu.VMEM((2, ATTN_TILE, LANES), F32)],
        compiler_params=_params(2),
        name="fox_attention",
    )(q, k, v, cum_cols, cum_rows)


def _odd_out_kernel(x_ref, yc_ref, yd_ref, wout_ref, g_ref, b_ref, o_ref):
    mix = _dot(yc_ref[...], wout_ref[:D_C, :]) + _dot(yd_ref[...], wout_ref[D_C:, :])
    o_ref[...] = _layer_norm(ALPHA * x_ref[...] + mix, g_ref[...], b_ref[...])


def _odd_out(x, y_c, y_d, w_out, g, b, i, layer):
    n = x.shape[0]
    row = lambda width: pl.BlockSpec((ROW_TILE, width), lambda t: (t, 0))
    return pl.pallas_call(
        _odd_out_kernel,
        grid=(n // ROW_TILE,),
        in_specs=[row(D_MODEL), row(D_C), row(D_D), _const_spec((D_C + D_D, D_MODEL), i),
                  _const_spec((1, D_MODEL), layer), _const_spec((1, D_MODEL), layer)],
        out_specs=row(D_MODEL),
        out_shape=jax.ShapeDtypeStruct((n, D_MODEL), F32),
        compiler_params=_params(1),
        name="odd_out",
    )(x, y_c, y_d, w_out, g, b)


def kernel(x, ev_w_in, ev_conv_w, ev_pool_w, ev_pool_scale, ev_w_out, od_w_in, od_forget_b,
           od_dw_w, od_dw_b, od_cn_g, od_cn_b, od_w_out, ln_mix_g, ln_mix_b, ln_ffn_g,
           ln_ffn_b, ffn_w_in, ffn_w_out):
    batch, seq, _ = x.shape
    assert seq % ROW_TILE == 0 and seq % ATTN_TILE == 0
    n_kv = seq // ATTN_TILE
    n_odd = od_w_in.shape[0]

    q_scale = 1.0 / math.sqrt(HEAD_DIM)
    od_w = jnp.concatenate(
        [od_w_in[:, :, :D_C] * q_scale, od_w_in[:, :, D_C:3 * D_C], od_w_in[:, :, 3 * D_C + H_C:],
         jnp.pad(od_w_in[:, :, 3 * D_C:3 * D_C + H_C], ((0, 0), (0, 0), (0, F_PAD - H_C)))],
        axis=-1).astype(BF16)
    od_fb = jnp.pad(od_forget_b, ((0, 0), (0, F_PAD - H_C)))[:, None, :]
    ev_w_in_b, ev_pool_w_b, ev_w_out_b = (w.astype(BF16) for w in (ev_w_in, ev_pool_w, ev_w_out))
    od_w_out_b, ffn_w_in_b, ffn_w_out_b = (w.astype(BF16) for w in (od_w_out, ffn_w_in, ffn_w_out))
    vec = lambda a: a[:, None, :]

    h = x.reshape(batch * seq, D_MODEL)
    for layer in range(DEPTH):
        i = layer // 2
        if layer % 2 == 0:
            h = _even_layer(h, ev_w_in_b, ev_conv_w, ev_pool_w_b, vec(ev_pool_scale), ev_w_out_b,
                            vec(ln_mix_g), vec(ln_mix_b), i, layer, seq)
        else:
            q, k, v, cum, y_d = _odd_in(h, od_w, od_fb, od_dw_w, vec(od_dw_b), vec(od_cn_g),
                                        vec(od_cn_b), i, seq)
            cum_rows = (cum[:, :H_C].reshape(batch, seq, H_C).transpose(0, 2, 1)
                        .reshape(batch, H_C * n_kv, 1, ATTN_TILE))
            y_c = _attention(q, k, v, cum, cum_rows, batch, seq)
            h = _odd_out(h, y_c, y_d, od_w_out_b, vec(ln_mix_g), vec(ln_mix_b), i, layer)
        h = _ffn(h, ffn_w_in_b, ffn_w_out_b, vec(ln_ffn_g), vec(ln_ffn_b), layer)
    return h.reshape(batch, seq, D_MODEL)
```

```python
import functools
import math

import jax
import jax.numpy as jnp
import numpy as np
from jax import lax
from jax.experimental import pallas as pl
from jax.experimental.pallas import tpu as pltpu

F32 = jnp.float32
BF16 = jnp.bfloat16

D_MODEL = 1024
DEPTH = 4
D_A = D_MODEL // 2
A_CONV = 3
D_B = D_MODEL // 2
POOL_WINDOWS = (2, 4, 8, 16)
POOL_GC = D_B // len(POOL_WINDOWS)
D_C = D_MODEL // 2
HEAD_DIM = 64
H_C = D_C // HEAD_DIM
D_D = D_MODEL // 2
D_CONV = 31
D_FF = int(math.ceil((8 * D_MODEL / 3) / 256) * 256)
ALPHA = (2.0 * DEPTH) ** 0.25
LN_EPS = 1e-5
MASK_VALUE = -1e30

LANES = 128
SUBLANES = 8
AUG_PIECES = 3
ROW_TILE = 512
ATTN_TILE = 512
FFN_CHUNK = 256
EVEN_HALO = 16
CONV_HALO = 32
F_PAD = LANES
OD_COLS = 3 * D_C + 2 * D_D + F_PAD
VMEM_LIMIT = 56 * 1024 * 1024


def _layer_norm(y, g, b):
    mu = jnp.mean(y, axis=-1, keepdims=True)
    d = y - mu
    var = jnp.mean(d * d, axis=-1, keepdims=True)
    return d * lax.rsqrt(var + LN_EPS) * g + b


def _dot(a, b):
    return jnp.dot(a, b, preferred_element_type=F32)


def _const_spec(shape, layer=None):
    if layer is None:
        return pl.BlockSpec(shape, lambda *_: (0,) * len(shape))
    return pl.BlockSpec((None,) + shape, lambda *_: (layer,) + (0,) * len(shape))


def _params(n_axes):
    return pltpu.CompilerParams(
        dimension_semantics=("arbitrary",) * n_axes, vmem_limit_bytes=VMEM_LIMIT)


def _ffn_kernel(x_ref, win_ref, wout_ref, g_ref, b_ref, o_ref, act_ref):
    x = x_ref[...]
    xb = x.astype(BF16)
    for c in range(D_FF // FFN_CHUNK):
        lo, hi = c * FFN_CHUNK, (c + 1) * FFN_CHUNK
        gate = _dot(xb, win_ref[:, lo:hi])
        up = _dot(xb, win_ref[:, D_FF + lo:D_FF + hi])
        act_ref[:, lo:hi] = (gate * jax.nn.sigmoid(gate) * up).astype(BF16)
    y = ALPHA * x + _dot(act_ref[...], wout_ref[...])
    o_ref[...] = _layer_norm(y, g_ref[...], b_ref[...])


def _ffn(x, w_in, w_out, g, b, layer):
    n = x.shape[0]
    row = pl.BlockSpec((ROW_TILE, D_MODEL), lambda i: (i, 0))
    return pl.pallas_call(
        _ffn_kernel,
        grid=(n // ROW_TILE,),
        in_specs=[row, _const_spec((D_MODEL, 2 * D_FF), layer), _const_spec((D_FF, D_MODEL), layer),
                  _const_spec((1, D_MODEL), layer), _const_spec((1, D_MODEL), layer)],
        out_specs=row,
        out_shape=jax.ShapeDtypeStruct((n, D_MODEL), F32),
        scratch_shapes=[pltpu.VMEM((ROW_TILE, D_FF), BF16)],
        compiler_params=_params(1),
        name="ffn",
    )(x, w_in, w_out, g, b)


def _carry_halo(ext_ref, first, halo, tm):
    @pl.when(first)
    def _():
        ext_ref[0:halo, :] = jnp.zeros((halo, ext_ref.shape[1]), ext_ref.dtype)

    @pl.when(jnp.logical_not(first))
    def _():
        ext_ref[0:halo, :] = ext_ref[tm:tm + halo, :]


def _even_kernel(x_ref, win_ref, convw_ref, poolw_ref, pscale_ref, wout_ref, g_ref, b_ref,
                 o_ref, ecv_ref, eu_ref, *, tiles_per_seq):
    tm, h0 = ROW_TILE, EVEN_HALO
    tile_in_seq = pl.program_id(0) % tiles_per_seq
    first = tile_in_seq == 0
    _carry_halo(ecv_ref, first, h0, tm)
    _carry_halo(eu_ref, first, h0, tm)

    x = x_ref[...]
    p = _dot(x.astype(BF16), win_ref[...])
    b_gate = p[:, :D_A]
    ecv_ref[h0:h0 + tm, :] = p[:, D_A:2 * D_A] * p[:, 2 * D_A:3 * D_A]
    eu_ref[h0:h0 + tm, :] = p[:, 3 * D_A:]

    conv = convw_ref[A_CONV - 1:A_CONV, :] * ecv_ref[h0:h0 + tm, :]
    for k in range(1, A_CONV):
        conv = conv + convw_ref[A_CONV - 1 - k:A_CONV - k, :] * ecv_ref[h0 - k:h0 - k + tm, :]
    y_a = (b_gate * conv).astype(BF16)

    pos1 = lax.broadcasted_iota(jnp.int32, (tm, POOL_GC), 0) + (tile_in_seq * tm + 1)
    y_b = []
    for gi, win in enumerate(POOL_WINDOWS):
        sl = slice(gi * POOL_GC, (gi + 1) * POOL_GC)
        cur = eu_ref[h0:h0 + tm, sl]
        tot = cur
        for k in range(1, win):
            tot = tot + eu_ref[h0 - k:h0 - k + tm, sl]
        cnt = jnp.minimum(pos1, win).astype(F32)
        pooled = (tot / cnt - cur).astype(BF16)
        y_b.append(_dot(pooled, poolw_ref[gi]) * pscale_ref[:, sl])
    y_b = jnp.concatenate(y_b, axis=-1).astype(BF16)

    mix = _dot(y_a, wout_ref[:D_A, :]) + _dot(y_b, wout_ref[D_A:, :])
    o_ref[...] = _layer_norm(ALPHA * x + mix, g_ref[...], b_ref[...])


def _even_layer(x, w_in, conv_w, pool_w, pool_scale, w_out, g, b, i, layer, seq):
    n = x.shape[0]
    row = pl.BlockSpec((ROW_TILE, D_MODEL), lambda t: (t, 0))
    return pl.pallas_call(
        functools.partial(_even_kernel, tiles_per_seq=seq // ROW_TILE),
        grid=(n // ROW_TILE,),
        in_specs=[row, _const_spec((D_MODEL, 3 * D_A + D_B), i), _const_spec((A_CONV, D_A), i),
                  _const_spec((len(POOL_WINDOWS), POOL_GC, POOL_GC), i), _const_spec((1, D_B), i),
                  _const_spec((D_A + D_B, D_MODEL), i),
                  _const_spec((1, D_MODEL), layer), _const_spec((1, D_MODEL), layer)],
        out_specs=row,
        out_shape=jax.ShapeDtypeStruct((n, D_MODEL), F32),
        scratch_shapes=[pltpu.VMEM((EVEN_HALO + ROW_TILE, D_A), F32),
                        pltpu.VMEM((EVEN_HALO + ROW_TILE, D_B), F32)],
        compiler_params=_params(1),
        name="even_mixer",
    )(x, w_in, conv_w, pool_w, pool_scale, w_out, g, b)


def _aug_placement():
    place = np.zeros((AUG_PIECES * F_PAD, 2 * D_C), np.float32)
    ones = np.zeros((1, 2 * D_C), np.float32)
    for head in range(H_C):
        base = (head // 2) * LANES + (HEAD_DIM if head % 2 == 0 else 0)
        for piece in range(AUG_PIECES):
            place[piece * F_PAD + head, base + piece] = 1.0
            place[piece * F_PAD + head, D_C + base + AUG_PIECES + piece] = -1.0
            ones[0, base + AUG_PIECES + piece] = 1.0
            ones[0, D_C + base + piece] = 1.0
    return jnp.asarray(place, BF16), jnp.asarray(ones, F32)


def _odd_in_kernel(x_ref, w_ref, fb_ref, dww_ref, dwb_ref, cng_ref, cnb_ref, place_ref, ones_ref,
                   q_ref, k_ref, v_ref, qa_ref, ka_ref, yd_ref, eh_ref, gs_ref, carry_ref,
                   *, tiles_per_seq):
    tm, h0 = ROW_TILE, CONV_HALO
    first = pl.program_id(0) % tiles_per_seq == 0
    _carry_halo(eh_ref, first, h0, tm)

    @pl.when(first)
    def _():
        carry_ref[...] = jnp.zeros(carry_ref.shape, F32)

    p = _dot(x_ref[...].astype(BF16), w_ref[...])
    q_ref[...] = p[:, :D_C].astype(BF16)
    k_ref[...] = p[:, D_C:2 * D_C].astype(BF16)
    v_ref[...] = p[:, 2 * D_C:3 * D_C].astype(BF16)

    c0 = 3 * D_C + 2 * D_D
    cum = jax.nn.log_sigmoid(p[:, c0:c0 + F_PAD] + fb_ref[...])
    rows = lax.broadcasted_iota(jnp.int32, (tm, F_PAD), 0)
    d = 1
    while d < tm:
        cum = cum + jnp.where(rows >= d, pltpu.roll(cum, d, axis=0), 0.0)
        d *= 2
    cum = cum + carry_ref[...]
    carry_ref[...] = cum[tm - 1:tm, :]

    pieces, rest = [], cum
    for _ in range(AUG_PIECES):
        piece = rest.astype(BF16)
        pieces.append(piece)
        rest = rest - piece.astype(F32)
    aug = _dot(jnp.concatenate(pieces, axis=-1), place_ref[...]) + ones_ref[...]
    qa_ref[...] = aug[:, :D_C].astype(BF16)
    ka_ref[...] = aug[:, D_C:].astype(BF16)

    a0 = 3 * D_C
    eh_ref[h0:h0 + tm, :] = p[:, a0:a0 + D_D] * jax.nn.sigmoid(p[:, a0 + D_D:a0 + 2 * D_D])
    rows_g = tm + SUBLANES
    acc = dwb_ref[...]
    for r in range(SUBLANES):
        part = None
        for a in range(-(-D_CONV // SUBLANES)):
            lag = SUBLANES * a + r
            if lag >= D_CONV:
                continue
            lo = h0 - SUBLANES * (a + 1)
            term = dww_ref[D_CONV - 1 - lag:D_CONV - lag, :] * eh_ref[lo:lo + rows_g, :]
            part = term if part is None else part + term
        if r == 0:
            acc = acc + part[SUBLANES:, :]
        else:
            gs_ref[r - 1] = part
            acc = acc + gs_ref[r - 1, SUBLANES - r:SUBLANES - r + tm, :]
    hn = _layer_norm(acc, cng_ref[...], cnb_ref[...])
    yd_ref[...] = (hn * jax.nn.sigmoid(hn)).astype(BF16)


def _odd_in(x, w, fb, dw_w, dw_b, cn_g, cn_b, i, seq):
    n = x.shape[0]
    row = lambda width: pl.BlockSpec((ROW_TILE, width), lambda t: (t, 0))
    half = jax.ShapeDtypeStruct((n, D_C), BF16)
    place, ones = _aug_placement()
    return pl.pallas_call(
        functools.partial(_odd_in_kernel, tiles_per_seq=seq // ROW_TILE),
        grid=(n // ROW_TILE,),
        in_specs=[row(D_MODEL), _const_spec((D_MODEL, OD_COLS), i), _const_spec((1, F_PAD), i),
                  _const_spec((D_CONV, D_D), i), _const_spec((1, D_D), i),
                  _const_spec((1, D_D), i), _const_spec((1, D_D), i),
                  _const_spec(place.shape), _const_spec(ones.shape)],
        out_specs=[row(D_C)] * 6,
        out_shape=[half] * 6,
        scratch_shapes=[pltpu.VMEM((CONV_HALO + ROW_TILE, D_D), F32),
                        pltpu.VMEM((SUBLANES - 1, ROW_TILE + SUBLANES, D_D), F32),
                        pltpu.VMEM((1, F_PAD), F32)],
        compiler_params=_params(1),
        name="odd_in",
    )(x, w, fb, dw_w, dw_b, cn_g, cn_b, place, ones)


def _attn_kernel(q_ref, k_ref, v_ref, qa_ref, ka_ref, o_ref, s0_ref, s1_ref, m_ref, acc_ref):
    tq = ATTN_TILE
    qi = pl.program_id(1)
    low = lax.broadcasted_iota(jnp.int32, (tq, LANES), 1) < HEAD_DIM
    causal = (lax.broadcasted_iota(jnp.int32, (tq, tq), 0)
              >= lax.broadcasted_iota(jnp.int32, (tq, tq), 1))

    def per_head(own, other):
        return [jnp.where(low, own, other), jnp.where(low, other, own)]

    for pair in range(H_C // 2):
        sl = slice(pair * LANES, (pair + 1) * LANES)
        qs = per_head(q_ref[:, sl], qa_ref[:, sl])
        for j in range(2):
            m_ref[j] = jnp.full((tq, LANES), MASK_VALUE, F32)
            acc_ref[j] = jnp.zeros((tq, LANES), F32)

        def logits(kv, s_ref, sl=sl, qs=qs):
            rows = pl.ds(pl.multiple_of(kv * tq, tq), tq)
            ks = per_head(k_ref[rows, sl], ka_ref[rows, sl])
            for j in range(2):
                s_ref[j] = lax.dot_general(qs[j], ks[j], (((1,), (1,)), ((), ())),
                                           preferred_element_type=F32)

        def softmax_pv(kv, s_ref, masked, sl=sl):
            vb = v_ref[pl.ds(pl.multiple_of(kv * tq, tq), tq), sl]
            vs = per_head(vb, jnp.ones_like(vb))
            for j in range(2):
                s = s_ref[j]
                if masked:
                    s = jnp.where(causal, s, MASK_VALUE)
                m_old = m_ref[j]
                m_new = jnp.maximum(m_old, jnp.max(s, axis=-1, keepdims=True))
                prob = jnp.exp(s - jnp.concatenate([m_new] * (tq // LANES), axis=1))
                acc_ref[j] = (acc_ref[j] * jnp.exp(m_old - m_new)
                              + _dot(prob.astype(BF16), vs[j]))
                m_ref[j] = m_new

        def body(i, carry):
            kv = 2 * i
            logits(kv + 1, s1_ref)
            softmax_pv(kv, s0_ref, False)
            logits(kv + 2, s0_ref)
            softmax_pv(kv + 1, s1_ref, False)
            return carry

        logits(0, s0_ref)
        lax.fori_loop(0, qi // 2, body, 0)

        @pl.when(qi % 2 == 0)
        def _():
            softmax_pv(qi, s0_ref, True)

        @pl.when(qi % 2 == 1)
        def _():
            logits(qi, s1_ref)
            softmax_pv(qi - 1, s0_ref, False)
            softmax_pv(qi, s1_ref, True)

        outs = []
        for j in range(2):
            a = acc_ref[j]
            outs.append(a / pltpu.roll(a, HEAD_DIM, axis=1))
        o_ref[:, sl] = jnp.where(low, outs[0], outs[1]).astype(BF16)


def _attention(q, k, v, q_aug, k_aug, batch, seq):
    n = q.shape[0]
    n_kv = seq // ATTN_TILE
    qrow = pl.BlockSpec((ATTN_TILE, D_C), lambda b, t: (b * n_kv + t, 0))
    kv_spec = pl.BlockSpec((seq, D_C), lambda b, t: (b, 0))
    return pl.pallas_call(
        _attn_kernel,
        grid=(batch, n_kv),
        in_specs=[qrow, kv_spec, kv_spec, qrow, kv_spec],
        out_specs=qrow,
        out_shape=jax.ShapeDtypeStruct((n, D_C), BF16),
        scratch_shapes=[pltpu.VMEM((2, ATTN_TILE, ATTN_TILE), F32),
                        pltpu.VMEM((2, ATTN_TILE, ATTN_TILE), F32),
                        pltpu.VMEM((2, ATTN_TILE, LANES), F32),
                        pltpu.VMEM((2, ATTN_TILE, LANES), F32)],
        compiler_params=_params(2),
        name="fox_attention",
    )(q, k, v, q_aug, k_aug)


def _odd_out_kernel(x_ref, yc_ref, yd_ref, wout_ref, g_ref, b_ref, o_ref):
    mix = _dot(yc_ref[...], wout_ref[:D_C, :]) + _dot(yd_ref[...], wout_ref[D_C:, :])
    o_ref[...] = _layer_norm(ALPHA * x_ref[...] + mix, g_ref[...], b_ref[...])


def _odd_out(x, y_c, y_d, w_out, g, b, i, layer):
    n = x.shape[0]
    row = lambda width: pl.BlockSpec((ROW_TILE, width), lambda t: (t, 0))
    return pl.pallas_call(
        _odd_out_kernel,
        grid=(n // ROW_TILE,),
        in_specs=[row(D_MODEL), row(D_C), row(D_D), _const_spec((D_C + D_D, D_MODEL), i),
                  _const_spec((1, D_MODEL), layer), _const_spec((1, D_MODEL), layer)],
        out_specs=row(D_MODEL),
        out_shape=jax.ShapeDtypeStruct((n, D_MODEL), F32),
        compiler_params=_params(1),
        name="odd_out",
    )(x, y_c, y_d, w_out, g, b)


def kernel(x, ev_w_in, ev_conv_w, ev_pool_w, ev_pool_scale, ev_w_out, od_w_in, od_forget_b,
           od_dw_w, od_dw_b, od_cn_g, od_cn_b, od_w_out, ln_mix_g, ln_mix_b, ln_ffn_g,
           ln_ffn_b, ffn_w_in, ffn_w_out):
    batch, seq, _ = x.shape
    assert seq % ROW_TILE == 0 and seq % ATTN_TILE == 0
    n_kv = seq // ATTN_TILE
    n_odd = od_w_in.shape[0]

    q_scale = 1.0 / math.sqrt(HEAD_DIM)
    od_w = jnp.concatenate(
        [od_w_in[:, :, :D_C] * q_scale, od_w_in[:, :, D_C:3 * D_C], od_w_in[:, :, 3 * D_C + H_C:],
         jnp.pad(od_w_in[:, :, 3 * D_C:3 * D_C + H_C], ((0, 0), (0, 0), (0, F_PAD - H_C)))],
        axis=-1).astype(BF16)
    od_fb = jnp.pad(od_forget_b, ((0, 0), (0, F_PAD - H_C)))[:, None, :]
    ev_w_in_b, ev_pool_w_b, ev_w_out_b = (w.astype(BF16) for w in (ev_w_in, ev_pool_w, ev_w_out))
    od_w_out_b, ffn_w_in_b, ffn_w_out_b = (w.astype(BF16) for w in (od_w_out, ffn_w_in, ffn_w_out))
    vec = lambda a: a[:, None, :]

    h = x.reshape(batch * seq, D_MODEL)
    for layer in range(DEPTH):
        i = layer // 2
        if layer % 2 == 0:
            h = _even_layer(h, ev_w_in_b, ev_conv_w, ev_pool_w_b, vec(ev_pool_scale), ev_w_out_b,
                            vec(ln_mix_g), vec(ln_mix_b), i, layer, seq)
        else:
            q, k, v, q_aug, k_aug, y_d = _odd_in(h, od_w, od_fb, od_dw_w, vec(od_dw_b),
                                                 vec(od_cn_g), vec(od_cn_b), i, seq)
            y_c = _attention(q, k, v, q_aug, k_aug, batch, seq)
            h = _odd_out(h, y_c, y_d, od_w_out_b, vec(ln_mix_g), vec(ln_mix_b), i, layer)
        h = _ffn(h, ffn_w_in_b, ffn_w_out_b, vec(ln_ffn_g), vec(ln_ffn_b), layer)
    return h.reshape(batch, seq, D_MODEL)
```

```python
import functools
import math

import jax
import jax.numpy as jnp
import numpy as np
from jax import lax
from jax.experimental import pallas as pl
from jax.experimental.pallas import tpu as pltpu

F32 = jnp.float32
BF16 = jnp.bfloat16

D_MODEL = 1024
DEPTH = 4
D_A = D_MODEL // 2
A_CONV = 3
D_B = D_MODEL // 2
POOL_WINDOWS = (2, 4, 8, 16)
POOL_GC = D_B // len(POOL_WINDOWS)
D_C = D_MODEL // 2
HEAD_DIM = 64
H_C = D_C // HEAD_DIM
D_D = D_MODEL // 2
D_CONV = 31
D_FF = int(math.ceil((8 * D_MODEL / 3) / 256) * 256)
ALPHA = (2.0 * DEPTH) ** 0.25
LN_EPS = 1e-5
MASK_VALUE = -1e30

LANES = 128
SUBLANES = 8
AUG_PIECES = 3
ROW_TILE = 512
ATTN_TILE = 512
PRUNE_MARGIN = 30.0
NORM_SLACK = 1.01
STAT_QNORM, STAT_KNORM, STAT_DIAG, STAT_CUM_FIRST, STAT_CUM_LAST, N_STATS = 0, 1, 2, 3, 4, 5
MXU_COLS = 256
FFN_CHUNK = MXU_COLS
FFN_SUBTILES = 2
EVEN_SUBTILES = 4
FFN_TIE_CHUNK = 3
EVEN_HALO = 16
CONV_HALO = 32
CONV_CHUNK = 128
F_PAD = LANES
OD_QKV0 = 2 * D_D + F_PAD
OD_COLS = OD_QKV0 + 3 * D_C
VMEM_LIMIT = 56 * 1024 * 1024


def _layer_norm(y, g, b):
    mu = jnp.mean(y, axis=-1, keepdims=True)
    d = y - mu
    var = jnp.mean(d * d, axis=-1, keepdims=True)
    return d * lax.rsqrt(var + LN_EPS) * g + b


def _dot(a, b):
    return jnp.dot(a, b, preferred_element_type=F32)


def _const_spec(shape, layer=None):
    once = pl.Buffered(1)
    if layer is None:
        return pl.BlockSpec(shape, lambda *_: (0,) * len(shape), pipeline_mode=once)
    return pl.BlockSpec((None,) + shape, lambda *_: (layer,) + (0,) * len(shape),
                        pipeline_mode=once)


def _params(n_axes):
    return pltpu.CompilerParams(
        dimension_semantics=("arbitrary",) * n_axes, vmem_limit_bytes=VMEM_LIMIT)


def _exact_zero(value, zero_ref):
    last = value[value.shape[0] - SUBLANES:, value.shape[1] - LANES:]
    return pltpu.bitcast(pltpu.bitcast(last, jnp.int32) & zero_ref[...], F32)


def _swiglu_ln(load_x, computed_x, zero_ref, win_ref, wout_ref, g_ref, b_ref, o_ref, act_ref):
    tie = None
    xs = [load_x(slice(t * ROW_TILE, (t + 1) * ROW_TILE)) for t in range(FFN_SUBTILES)]
    for t in range(FFN_SUBTILES):
        rows = slice(t * ROW_TILE, (t + 1) * ROW_TILE)
        x = xs[t]
        xb = x.astype(BF16)
        for c in range(D_FF // FFN_CHUNK):
            lo, hi = c * FFN_CHUNK, (c + 1) * FFN_CHUNK
            gate = _dot(xb, win_ref[:, lo:hi])
            up = _dot(xb, win_ref[:, D_FF + lo:D_FF + hi])
            act = gate * jax.nn.sigmoid(gate) * up
            if tie is not None and c == FFN_TIE_CHUNK:
                act = act + jnp.tile(tie, (ROW_TILE // SUBLANES, FFN_CHUNK // LANES))
            act_ref[t, :, lo:hi] = act.astype(BF16)
        y = ALPHA * x + _dot(act_ref[t], wout_ref[...])
        if computed_x and t + 1 < FFN_SUBTILES:
            y = y + jnp.tile(_exact_zero(xs[t + 1], zero_ref),
                             (ROW_TILE // SUBLANES, D_MODEL // LANES))
        out = _layer_norm(y, g_ref[...], b_ref[...])
        o_ref[rows, :] = out
        tie = _exact_zero(out, zero_ref)


def _ffn_kernel(x_ref, zero_ref, *rest):
    _swiglu_ln(lambda rows: x_ref[rows, :], False, zero_ref, *rest)


def _mix_ffn_kernel(x_ref, yc_ref, yd_ref, wmix_ref, gmix_ref, bmix_ref, zero_ref, *rest):
    def load_x(rows):
        mix = (_dot(yc_ref[rows, :], wmix_ref[:D_C, :])
               + _dot(yd_ref[rows, :], wmix_ref[D_C:, :]))
        return _layer_norm(ALPHA * x_ref[rows, :] + mix, gmix_ref[...], bmix_ref[...])

    _swiglu_ln(load_x, True, zero_ref, *rest)


def _ffn(x, w_in, w_out, g, b, layer, mix=None):
    n = x.shape[0]
    tile = FFN_SUBTILES * ROW_TILE
    row = lambda width: pl.BlockSpec((tile, width), lambda t: (t, 0))
    args, specs, body = [x], [row(D_MODEL)], _ffn_kernel
    if mix is not None:
        y_c, y_d, w_mix, g_mix, b_mix, i = mix
        args += [y_c, y_d, w_mix, g_mix, b_mix]
        specs += [row(D_C), row(D_D), _const_spec((D_C + D_D, D_MODEL), i),
                  _const_spec((1, D_MODEL), layer), _const_spec((1, D_MODEL), layer)]
        body = _mix_ffn_kernel
    return pl.pallas_call(
        body,
        grid=(n // tile,),
        in_specs=specs + [_const_spec((SUBLANES, LANES)),
                          _const_spec((D_MODEL, 2 * D_FF), layer),
                          _const_spec((D_FF, D_MODEL), layer),
                          _const_spec((1, D_MODEL), layer), _const_spec((1, D_MODEL), layer)],
        out_specs=row(D_MODEL),
        out_shape=jax.ShapeDtypeStruct((n, D_MODEL), F32),
        scratch_shapes=[pltpu.VMEM((FFN_SUBTILES, ROW_TILE, D_FF), BF16)],
        compiler_params=_params(1),
        name="ffn" if mix is None else "mix_ffn",
    )(*args, jnp.zeros((SUBLANES, LANES), jnp.int32), w_in, w_out, g, b)


def _carry_halo(ext_ref, first, halo, tm):
    @pl.when(first)
    def _():
        ext_ref[0:halo, :] = jnp.zeros((halo, ext_ref.shape[1]), ext_ref.dtype)

    @pl.when(jnp.logical_not(first))
    def _():
        ext_ref[0:halo, :] = ext_ref[tm:tm + halo, :]


def _even_kernel(x_ref, win_ref, convw_ref, poolw_ref, pscale_ref, wout_ref, g_ref, b_ref,
                 zero_ref, o_ref, ecv_ref, eu_ref, *, tiles_per_seq):
    tm, h0 = ROW_TILE, EVEN_HALO
    tile = EVEN_SUBTILES * tm
    tile_in_seq = pl.program_id(0) % tiles_per_seq
    first = tile_in_seq == 0
    _carry_halo(ecv_ref, first, h0, tile)
    _carry_halo(eu_ref, first, h0, tile)

    def in_proj(r0):
        x = x_ref[r0:r0 + tm, :]
        p = _dot(x.astype(BF16), win_ref[...])
        ecv_ref[h0 + r0:h0 + r0 + tm, :] = p[:, D_A:2 * D_A] * p[:, 2 * D_A:3 * D_A]
        eu_ref[h0 + r0:h0 + r0 + tm, :] = p[:, 3 * D_A:]
        return x, p[:, :D_A], _exact_zero(p, zero_ref)

    def mixers(r0, b_gate, tie):
        e0 = h0 + r0
        conv = convw_ref[A_CONV - 1:A_CONV, :] * ecv_ref[e0:e0 + tm, :]
        for k in range(1, A_CONV):
            conv = conv + (convw_ref[A_CONV - 1 - k:A_CONV - k, :]
                           * ecv_ref[e0 - k:e0 - k + tm, :])
        y_a = (b_gate * conv).astype(BF16)
        pos1 = (lax.broadcasted_iota(jnp.int32, (tm, POOL_GC), 0)
                + (tile_in_seq * tile + r0 + 1))
        y_b = []
        for gi, win in enumerate(POOL_WINDOWS):
            sl = slice(gi * POOL_GC, (gi + 1) * POOL_GC)
            cur = eu_ref[e0:e0 + tm, sl]
            tot = cur
            for k in range(1, win):
                tot = tot + eu_ref[e0 - k:e0 - k + tm, sl]
            cnt = jnp.minimum(pos1, win).astype(F32)
            pooled = (tot / cnt - cur).astype(BF16)
            y_b.append(_dot(pooled, poolw_ref[gi]) * pscale_ref[:, sl])
        if tie is not None:
            y_b[-1] = y_b[-1] + jnp.tile(tie, (tm // SUBLANES, 1))
        return y_a, jnp.concatenate(y_b, axis=-1).astype(BF16)

    def out_proj(y_a, y_b):
        return _dot(y_a, wout_ref[:D_A, :]) + _dot(y_b, wout_ref[D_A:, :])

    projected = [in_proj(0)]
    mix_done, out_done = None, None
    for t in range(EVEN_SUBTILES):
        x, gate, _ = projected[t]
        tie = mix_done
        if t + 1 < EVEN_SUBTILES:
            projected.append(in_proj((t + 1) * tm))
            tie = projected[t + 1][2] if tie is None else tie + projected[t + 1][2]
        mix = out_proj(*mixers(t * tm, gate, tie))
        y = ALPHA * x + mix
        if out_done is not None:
            y = y + jnp.tile(out_done, (tm // SUBLANES, D_MODEL // LANES))
        out = _layer_norm(y, g_ref[...], b_ref[...])
        o_ref[t * tm:(t + 1) * tm, :] = out
        mix_done, out_done = _exact_zero(mix, zero_ref), _exact_zero(out, zero_ref)


def _even_layer(x, w_in, conv_w, pool_w, pool_scale, w_out, g, b, i, layer, seq):
    n = x.shape[0]
    tile = EVEN_SUBTILES * ROW_TILE
    row = pl.BlockSpec((tile, D_MODEL), lambda t: (t, 0))
    return pl.pallas_call(
        functools.partial(_even_kernel, tiles_per_seq=seq // tile),
        grid=(n // tile,),
        in_specs=[row, _const_spec((D_MODEL, 3 * D_A + D_B), i), _const_spec((A_CONV, D_A), i),
                  _const_spec((len(POOL_WINDOWS), POOL_GC, POOL_GC), i), _const_spec((1, D_B), i),
                  _const_spec((D_A + D_B, D_MODEL), i),
                  _const_spec((1, D_MODEL), layer), _const_spec((1, D_MODEL), layer),
                  _const_spec((SUBLANES, LANES))],
        out_specs=row,
        out_shape=jax.ShapeDtypeStruct((n, D_MODEL), F32),
        scratch_shapes=[pltpu.VMEM((EVEN_HALO + tile, D_A), F32),
                        pltpu.VMEM((EVEN_HALO + tile, D_B), F32)],
        compiler_params=_params(1),
        name="even_mixer",
    )(x, w_in, conv_w, pool_w, pool_scale, w_out, g, b, jnp.zeros((SUBLANES, LANES), jnp.int32))


def _aug_placement():
    place = np.zeros((AUG_PIECES * F_PAD, 2 * D_C), np.float32)
    ones = np.zeros((1, 2 * D_C), np.float32)
    for head in range(H_C):
        base = (head // 2) * LANES + (HEAD_DIM if head % 2 == 0 else 0)
        for piece in range(AUG_PIECES):
            place[piece * F_PAD + head, base + piece] = 1.0
            place[piece * F_PAD + head, D_C + base + AUG_PIECES + piece] = -1.0
            ones[0, base + AUG_PIECES + piece] = 1.0
            ones[0, D_C + base + piece] = 1.0
    return jnp.asarray(place, BF16), jnp.asarray(ones, F32)


def _odd_in_kernel(x_ref, w_ref, fb_ref, dww_ref, dwb_ref, cng_ref, cnb_ref, place_ref, ones_ref,
                   zero_ref, seg_ref, q_ref, k_ref, v_ref, qa_ref, ka_ref, yd_ref, stats_ref,
                   eh_ref, carry_ref, *, tiles_per_seq):
    tm, h0 = ROW_TILE, CONV_HALO
    first = pl.program_id(0) % tiles_per_seq == 0
    _carry_halo(eh_ref, first, h0, tm)

    @pl.when(first)
    def _():
        carry_ref[...] = jnp.zeros(carry_ref.shape, F32)

    xb = x_ref[...].astype(BF16)
    p = _dot(xb, w_ref[:, :OD_QKV0])
    sq_norms = ([], [])

    def qkv_columns(piece):
        lo = piece * MXU_COLS
        cols = _dot(xb, w_ref[:, OD_QKV0 + lo:OD_QKV0 + lo + MXU_COLS])
        out_ref = (q_ref, k_ref, v_ref)[lo // D_C]
        out_ref[:, lo % D_C:lo % D_C + MXU_COLS] = cols.astype(BF16)
        if lo // D_C < 2:
            sq_norms[lo // D_C].append(
                _dot((cols * cols).astype(BF16), seg_ref[lo % D_C:lo % D_C + MXU_COLS, :]))
        return _exact_zero(cols, zero_ref)

    cum = jax.nn.log_sigmoid(p[:, 2 * D_D:] + fb_ref[...])
    rows = lax.broadcasted_iota(jnp.int32, (tm, F_PAD), 0)
    d = 1
    while d < tm:
        cum = cum + jnp.where(rows >= d, pltpu.roll(cum, d, axis=0), 0.0)
        d *= 2
    cum = cum + carry_ref[...]
    carry_ref[...] = cum[tm - 1:tm, :]

    pieces, rest = [], cum
    for _ in range(AUG_PIECES):
        piece = rest.astype(BF16)
        pieces.append(piece)
        rest = rest - piece.astype(F32)
    aug = _dot(jnp.concatenate(pieces, axis=-1), place_ref[...]) + ones_ref[...]
    qa_ref[...] = aug[:, :D_C].astype(BF16)
    ka_ref[...] = aug[:, D_C:].astype(BF16)

    eh_ref[h0:h0 + tm, :] = p[:, :D_D] * jax.nn.sigmoid(p[:, D_D:2 * D_D])
    n_shift = -(-D_CONV // SUBLANES)
    reach = SUBLANES * n_shift
    n_pieces = 3 * D_C // MXU_COLS
    per_chunk = -(-n_pieces // (tm // CONV_CHUNK))
    for r0 in range(0, tm, CONV_CHUNK):
        anchor = None
        chunk = r0 // CONV_CHUNK
        for piece in range(chunk * per_chunk, min((chunk + 1) * per_chunk, n_pieces)):
            zero = qkv_columns(piece)
            anchor = zero if anchor is None else anchor + zero
        if anchor is not None:
            anchor = jnp.tile(anchor, (CONV_CHUNK // SUBLANES, 1))
        tiles = []
        for lt in range(D_D // LANES):
            ls = slice(lt * LANES, (lt + 1) * LANES)
            blk = eh_ref[h0 + r0 - reach:h0 + r0 + CONV_CHUNK, ls]
            acc = jnp.broadcast_to(dwb_ref[:, ls], (CONV_CHUNK, LANES))
            for r in range(SUBLANES):
                part = None
                for a in range(n_shift):
                    lag = SUBLANES * a + r
                    if lag >= D_CONV:
                        continue
                    lo = reach - SUBLANES * (a + 1)
                    term = (dww_ref[D_CONV - 1 - lag:D_CONV - lag, ls]
                            * blk[lo:lo + CONV_CHUNK + SUBLANES, :])
                    part = term if part is None else part + term
                if r:
                    part = pltpu.roll(part, r, axis=0)
                acc = acc + part[SUBLANES:, :]
            if anchor is not None and lt == D_D // LANES - 1:
                acc = acc + anchor
            tiles.append(acc)
        hn = _layer_norm(jnp.concatenate(tiles, axis=-1), cng_ref[...], cnb_ref[...])
        yd_ref[r0:r0 + CONV_CHUNK, :] = (hn * jax.nn.sigmoid(hn)).astype(BF16)

    q_sq, k_sq = (sum(parts) for parts in sq_norms)
    row_max = lambda a, slack: jnp.sqrt(jnp.max(a, axis=0, keepdims=True)) * slack
    stats_ref[...] = jnp.concatenate(
        [row_max(q_sq, NORM_SLACK), row_max(k_sq, NORM_SLACK),
         row_max(q_sq * k_sq, NORM_SLACK * NORM_SLACK), cum[0:1, :], cum[tm - 1:tm, :],
         jnp.zeros((SUBLANES - N_STATS, F_PAD), F32)], axis=0)


def _odd_in(x, w, fb, dw_w, dw_b, cn_g, cn_b, i, seq):
    n = x.shape[0]
    row = lambda width: pl.BlockSpec((ROW_TILE, width), lambda t: (t, 0))
    half = jax.ShapeDtypeStruct((n, D_C), BF16)
    place, ones = _aug_placement()
    seg = jnp.asarray(np.arange(D_C)[:, None] // HEAD_DIM == np.arange(F_PAD)[None, :], BF16)
    n_tiles = n // ROW_TILE
    return pl.pallas_call(
        functools.partial(_odd_in_kernel, tiles_per_seq=seq // ROW_TILE),
        grid=(n_tiles,),
        in_specs=[row(D_MODEL), _const_spec((D_MODEL, OD_COLS), i), _const_spec((1, F_PAD), i),
                  _const_spec((D_CONV, D_D), i), _const_spec((1, D_D), i),
                  _const_spec((1, D_D), i), _const_spec((1, D_D), i),
                  _const_spec(place.shape), _const_spec(ones.shape),
                  _const_spec((SUBLANES, LANES)), _const_spec(seg.shape)],
        out_specs=[row(D_C)] * 6 + [pl.BlockSpec((None, SUBLANES, F_PAD), lambda t: (t, 0, 0))],
        out_shape=[half] * 6 + [jax.ShapeDtypeStruct((n_tiles, SUBLANES, F_PAD), F32)],
        scratch_shapes=[pltpu.VMEM((CONV_HALO + ROW_TILE, D_D), F32),
                        pltpu.VMEM((1, F_PAD), F32)],
        compiler_params=_params(1),
        name="odd_in",
    )(x, w, fb, dw_w, dw_b, cn_g, cn_b, place, ones, jnp.zeros((SUBLANES, LANES), jnp.int32),
      seg)


def _first_needed_key_tile(stats_ref, bi, qi, pair, n_kv):
    stat = lambda tile, kind, head: stats_ref[((bi * n_kv + tile) * N_STATS + kind) * H_C + head]
    heads = (2 * pair, 2 * pair + 1)
    reach = [stat(qi, STAT_QNORM, h) for h in heads]
    floor = [stat(qi, STAT_DIAG, h) + stat(qi, STAT_CUM_FIRST, h) for h in heads]
    start = jnp.int32(0)
    all_dropped = jnp.bool_(True)
    for j in range(n_kv):
        droppable = j < qi
        for x, h in enumerate(heads):
            gap = reach[x] * stat(j, STAT_KNORM, h) + floor[x] - stat(j, STAT_CUM_LAST, h)
            droppable = jnp.logical_and(droppable, gap < -PRUNE_MARGIN)
        all_dropped = jnp.logical_and(all_dropped, droppable)
        start = start + all_dropped.astype(jnp.int32)
    return start


def _attn_kernel(stats_ref, q_ref, k_ref, v_ref, qa_ref, ka_ref, o_ref, s0_ref, s1_ref, m_ref,
                 acc_ref, *, n_kv):
    tq = ATTN_TILE
    bi, qi = pl.program_id(0), pl.program_id(1)
    low = lax.broadcasted_iota(jnp.int32, (tq, LANES), 1) < HEAD_DIM
    causal = (lax.broadcasted_iota(jnp.int32, (tq, tq), 0)
              >= lax.broadcasted_iota(jnp.int32, (tq, tq), 1))

    def per_head(own, other):
        return [jnp.where(low, own, other), jnp.where(low, other, own)]

    for pair in range(H_C // 2):
        sl = slice(pair * LANES, (pair + 1) * LANES)
        qs = per_head(q_ref[:, sl], qa_ref[:, sl])
        for j in range(2):
            m_ref[j] = jnp.full((tq, LANES), MASK_VALUE, F32)
            acc_ref[j] = jnp.zeros((tq, LANES), F32)

        def logits(kv, s_ref, sl=sl, qs=qs):
            rows = pl.ds(pl.multiple_of(kv * tq, tq), tq)
            ks = per_head(k_ref[rows, sl], ka_ref[rows, sl])
            for j in range(2):
                s_ref[j] = lax.dot_general(qs[j], ks[j], (((1,), (1,)), ((), ())),
                                           preferred_element_type=F32)

        def softmax_pv(kv, s_ref, masked, sl=sl):
            vb = v_ref[pl.ds(pl.multiple_of(kv * tq, tq), tq), sl]
            vs = per_head(vb, jnp.ones_like(vb))
            for j in range(2):
                s = s_ref[j]
                if masked:
                    s = jnp.where(causal, s, MASK_VALUE)
                m_old = m_ref[j]
                m_new = jnp.maximum(m_old, jnp.max(s, axis=-1, keepdims=True))
                shifted = s - jnp.concatenate([m_new] * (tq // LANES), axis=1)
                prob = jnp.exp(shifted.astype(BF16))
                acc_ref[j] = acc_ref[j] * jnp.exp(m_old - m_new) + _dot(prob, vs[j])
                m_ref[j] = m_new

        start = _first_needed_key_tile(stats_ref, bi, qi, pair, n_kv)
        unmasked = qi - start

        def body(i, carry, start=start):
            kv = start + 2 * i
            logits(kv + 1, s1_ref)
            softmax_pv(kv, s0_ref, False)
            logits(kv + 2, s0_ref)
            softmax_pv(kv + 1, s1_ref, False)
            return carry

        logits(start, s0_ref)
        lax.fori_loop(0, unmasked // 2, body, 0)

        @pl.when(unmasked % 2 == 0)
        def _():
            softmax_pv(qi, s0_ref, True)

        @pl.when(unmasked % 2 == 1)
        def _():
            logits(qi, s1_ref)
            softmax_pv(qi - 1, s0_ref, False)
            softmax_pv(qi, s1_ref, True)

        outs = []
        for j in range(2):
            a = acc_ref[j]
            outs.append(a / pltpu.roll(a, HEAD_DIM, axis=1))
        o_ref[:, sl] = jnp.where(low, outs[0], outs[1]).astype(BF16)


def _attention(q, k, v, q_aug, k_aug, stats, batch, seq):
    n = q.shape[0]
    n_kv = seq // ATTN_TILE
    qrow = pl.BlockSpec((ATTN_TILE, D_C), lambda b, t, stats: (b * n_kv + t, 0))
    kv_spec = pl.BlockSpec((seq, D_C), lambda b, t, stats: (b, 0))
    return pl.pallas_call(
        functools.partial(_attn_kernel, n_kv=n_kv),
        grid_spec=pltpu.PrefetchScalarGridSpec(
            num_scalar_prefetch=1,
            grid=(batch, n_kv),
            in_specs=[qrow, kv_spec, kv_spec, qrow, kv_spec],
            out_specs=qrow,
            scratch_shapes=[pltpu.VMEM((2, ATTN_TILE, ATTN_TILE), F32),
                            pltpu.VMEM((2, ATTN_TILE, ATTN_TILE), F32),
                            pltpu.VMEM((2, ATTN_TILE, LANES), F32),
                            pltpu.VMEM((2, ATTN_TILE, LANES), F32)]),
        out_shape=jax.ShapeDtypeStruct((n, D_C), BF16),
        compiler_params=_params(2),
        name="fox_attention",
    )(stats[:, :N_STATS, :H_C].reshape(-1), q, k, v, q_aug, k_aug)


def kernel(x, ev_w_in, ev_conv_w, ev_pool_w, ev_pool_scale, ev_w_out, od_w_in, od_forget_b,
           od_dw_w, od_dw_b, od_cn_g, od_cn_b, od_w_out, ln_mix_g, ln_mix_b, ln_ffn_g,
           ln_ffn_b, ffn_w_in, ffn_w_out):
    batch, seq, _ = x.shape
    assert ROW_TILE == ATTN_TILE, "odd_in's per-tile statistics are per attention tile"
    assert seq % (EVEN_SUBTILES * ROW_TILE) == 0 and seq % ATTN_TILE == 0
    assert (batch * seq) % (FFN_SUBTILES * ROW_TILE) == 0

    q_scale = 1.0 / math.sqrt(HEAD_DIM)
    n_odd = od_w_in.shape[0]
    order = jnp.argsort(od_forget_b, axis=-1)

    def by_head(cols):
        heads = cols.reshape(n_odd, D_MODEL, H_C, HEAD_DIM)
        heads = jnp.take_along_axis(heads, order[:, None, :, None], axis=2)
        return heads.reshape(n_odd, D_MODEL, D_C)

    w_f = jnp.take_along_axis(od_w_in[:, :, 3 * D_C:3 * D_C + H_C], order[:, None, :], axis=2)
    od_w = jnp.concatenate(
        [od_w_in[:, :, 3 * D_C + H_C:],
         jnp.pad(w_f, ((0, 0), (0, 0), (0, F_PAD - H_C))),
         by_head(od_w_in[:, :, :D_C]) * q_scale, by_head(od_w_in[:, :, D_C:2 * D_C]),
         by_head(od_w_in[:, :, 2 * D_C:3 * D_C])],
        axis=-1).astype(BF16)
    od_fb = jnp.pad(jnp.take_along_axis(od_forget_b, order, axis=1),
                    ((0, 0), (0, F_PAD - H_C)))[:, None, :]
    w_out_c = od_w_out[:, :D_C, :].reshape(n_odd, H_C, HEAD_DIM, D_MODEL)
    w_out_c = jnp.take_along_axis(w_out_c, order[:, :, None, None], axis=1)
    od_w_out_b = jnp.concatenate(
        [w_out_c.reshape(n_odd, D_C, D_MODEL), od_w_out[:, D_C:, :]], axis=1).astype(BF16)
    ev_w_in_b, ev_pool_w_b, ev_w_out_b = (w.astype(BF16) for w in (ev_w_in, ev_pool_w, ev_w_out))
    ffn_w_in_b, ffn_w_out_b = ffn_w_in.astype(BF16), ffn_w_out.astype(BF16)
    vec = lambda a: a[:, None, :]

    h = x.reshape(batch * seq, D_MODEL)
    for layer in range(DEPTH):
        i = layer // 2
        mix = None
        if layer % 2 == 0:
            h = _even_layer(h, ev_w_in_b, ev_conv_w, ev_pool_w_b, vec(ev_pool_scale), ev_w_out_b,
                            vec(ln_mix_g), vec(ln_mix_b), i, layer, seq)
        else:
            q, k, v, q_aug, k_aug, y_d, stats = _odd_in(h, od_w, od_fb, od_dw_w, vec(od_dw_b),
                                                        vec(od_cn_g), vec(od_cn_b), i, seq)
            y_c = _attention(q, k, v, q_aug, k_aug, stats, batch, seq)
            mix = (y_c, y_d, od_w_out_b, vec(ln_mix_g), vec(ln_mix_b), i)
        h = _ffn(h, ffn_w_in_b, ffn_w_out_b, vec(ln_ffn_g), vec(ln_ffn_b), layer, mix)
    return h.reshape(batch, seq, D_MODEL)
```
